```python
import math
import jax, jax.numpy as jnp
from jax import lax
import numpy as np

D_MODEL = 1024
BATCH = 32
SEQ = 2048
DEPTH = 1
DEC_BATCH = 16
DEC_SEQ = 4096
PAST_LEN = 128

GRID_W = 64
NA_WIDTH = D_MODEL // 2
NA_HEAD_DIM = 64
NA_HEADS = NA_WIDTH // NA_HEAD_DIM
NA_KH = 8
NA_KW = 16
NA_QB = 16
NA_KB = 32
RWKV_WIDTH = D_MODEL - NA_WIDTH
RWKV_HEAD_DIM = 64
RWKV_HEADS = RWKV_WIDTH // RWKV_HEAD_DIM
DECAY_LORA = 64
ICLR_LORA = 64
GATE_LORA = 128
RWKV_SPLITS = (RWKV_WIDTH, RWKV_WIDTH, RWKV_WIDTH, DECAY_LORA, DECAY_LORA, ICLR_LORA, ICLR_LORA, GATE_LORA)
RWKV_IN = sum(RWKV_SPLITS)
IN_WIDTH = 3 * NA_WIDTH + RWKV_IN
N_GROUPS = 4
EXPERTS_PER_GROUP = 8
N_EXPERTS = N_GROUPS * EXPERTS_PER_GROUP
EXPERT_HIDDEN = 512
TOP_K_IN_GROUP = 2
MOE_BLOCK = 256
NORM_EPS = 1e-6
LNX_EPS = 64e-5
MASK_VALUE = -1e30

kernel_name = 'hybrid_na_rwkv7_hmoe_encoder'


def rms_norm(x, g):
    xf = x.astype(jnp.float32)
    y = xf * lax.rsqrt(jnp.mean(xf * xf, axis=-1, keepdims=True) + NORM_EPS)
    return (y * g.astype(jnp.float32)).astype(x.dtype)


def _na_column_tables():
    nqb = GRID_W // NA_QB
    qcols = np.arange(GRID_W).reshape(nqb, NA_QB)
    band_start = np.clip(np.arange(nqb) * NA_QB - NA_KW // 2, 0, GRID_W - NA_KB)
    colidx = band_start[:, None] + np.arange(NA_KB)[None, :]
    win_start = np.clip(qcols - NA_KW // 2, 0, GRID_W - NA_KW)
    kc = colidx[:, None, :]
    valid = (kc >= win_start[..., None]) & (kc < win_start[..., None] + NA_KW)
    dxidx = np.clip(kc - qcols[..., None] + NA_KW - 1, 0, 2 * NA_KW - 2)
    return colidx.astype(np.int32), valid, dxidx.astype(np.int32)


def neighbourhood_attention(q, k, v, rpb):
    bsz, seq, _ = q.shape
    rows = seq // GRID_W
    kh = min(NA_KH, rows)
    nqb = GRID_W // NA_QB
    colidx, valid, dxidx = _na_column_tables()
    shp = (bsz, rows, GRID_W, NA_HEADS, NA_HEAD_DIM)
    q, k, v = q.reshape(shp), k.reshape(shp), v.reshape(shp)
    rpb_t = jnp.transpose(rpb[:, :, dxidx], (0, 2, 3, 1, 4)).astype(jnp.float32)
    col_mask = jnp.asarray(np.where(valid, 0.0, MASK_VALUE), jnp.float32)[:, :, None, :]
    colidx_j = jnp.asarray(colidx)
    scale = NA_HEAD_DIM ** -0.5

    def row_step(args):
        i, q_row = args
        rs = jnp.clip(i - kh // 2, 0, rows - kh)
        k_r = lax.dynamic_slice_in_dim(k, rs, kh, axis=1)[:, :, colidx_j]
        v_r = lax.dynamic_slice_in_dim(v, rs, kh, axis=1)[:, :, colidx_j]
        dy = rs + jnp.arange(kh) - i + (NA_KH - 1)
        bias = rpb_t[:, :, :, dy, :]
        qb = q_row.reshape(bsz, nqb, NA_QB, NA_HEADS, NA_HEAD_DIM)
        s = jnp.einsum('bnqhd,bmnkhd->bhnqmk', qb, k_r).astype(jnp.float32) * scale + bias + col_mask
        p = jax.nn.softmax(s.reshape(bsz, NA_HEADS, nqb, NA_QB, kh * NA_KB), axis=-1).reshape(s.shape)
        o = jnp.einsum('bhnqmk,bmnkhd->bnqhd', p.astype(v.dtype), v_r)
        return o.reshape(bsz, GRID_W, NA_WIDTH)

    out = lax.map(row_step, (jnp.arange(rows), jnp.moveaxis(q, 1, 0)))
    return jnp.moveaxis(out, 0, 1).reshape(bsz, seq, NA_WIDTH)


def token_shift(z, mu):
    prev = jnp.pad(z[:, :-1], ((0, 0), (1, 0), (0, 0)))
    nxt = jnp.pad(z[:, 1:], ((0, 0), (0, 1), (0, 0)))
    return z + mu[0] * (prev - z) + mu[1] * (nxt - z)


def wkv7_scan(r, w, k, v, a, b, reverse):
    bsz, seq, nh, n = r.shape

    def step(S, inp):
        r_t, w_t, k_t, v_t, a_t, b_t = inp
        sa = jnp.einsum('bhvk,bhk->bhv', S, a_t)
        S = S * w_t[:, :, None, :] + sa[..., None] * b_t[:, :, None, :] + v_t[..., None] * k_t[:, :, None, :]
        return S, jnp.einsum('bhvk,bhk->bhv', S, r_t)

    S0 = jnp.zeros((bsz, nh, n, n), jnp.float32)
    xs = tuple(jnp.moveaxis(t, 1, 0) for t in (r, w, k, v, a, b))
    _, ys = lax.scan(step, S0, xs, reverse=reverse)
    return jnp.moveaxis(ys, 0, 1)


def rwkv7_direction(rh, k, vh, kk, xw, xa, w0, w2, a0, a2, k_a, r_k, reverse):
    bsz, seq, _ = k.shape
    heads = lambda t: t.reshape(bsz, seq, RWKV_HEADS, RWKV_HEAD_DIM)
    logw = -jax.nn.softplus(-(w0 + jnp.tanh(xw) @ w2)) - 0.5
    decay = jnp.exp(-jnp.exp(logw))
    a = jax.nn.sigmoid(a0 + xa @ a2)
    kdh = heads(k * (1.0 + (a - 1.0) * k_a))
    wkv = wkv7_scan(rh, heads(decay), kdh, vh, -kk, kk * heads(a), reverse)
    bonus = jnp.sum(rh * kdh * r_k, axis=-1, keepdims=True) * vh
    return wkv, bonus


def rwkv7_bidirectional(z, mu, w0, w2, a0, a2, g2, k_k, k_a, r_k, lnx_g, lnx_b):
    bsz, seq, _ = z.shape
    heads = lambda t: t.reshape(bsz, seq, RWKV_HEADS, RWKV_HEAD_DIM)
    z = token_shift(z.astype(jnp.float32), mu.astype(jnp.float32))
    cuts = [int(c) for c in np.cumsum(RWKV_SPLITS)[:-1]]
    r, k, v, xw_f, xw_b, xa_f, xa_b, xg = jnp.split(z, cuts, axis=-1)
    g = jax.nn.sigmoid(xg) @ g2.astype(jnp.float32)
    kk = heads(k * k_k)
    kk = kk / jnp.maximum(jnp.sqrt(jnp.sum(kk * kk, axis=-1, keepdims=True)), 1e-12)
    rh, vh = heads(r), heads(v)
    wkv_f, bonus_f = rwkv7_direction(rh, k, vh, kk, xw_f, xa_f, w0[0], w2[0], a0[0], a2[0], k_a, r_k, False)
    wkv_b, bonus_b = rwkv7_direction(rh, k, vh, kk, xw_b, xa_b, w0[1], w2[1], a0[1], a2[1], k_a, r_k, True)
    wkv = wkv_f + wkv_b
    mean = jnp.mean(wkv, axis=-1, keepdims=True)
    var = jnp.mean(jnp.square(wkv - mean), axis=-1, keepdims=True)
    y = ((wkv - mean) * lax.rsqrt(var + LNX_EPS)).reshape(bsz, seq, RWKV_WIDTH) * lnx_g + lnx_b
    y = y + (bonus_f + bonus_b).reshape(bsz, seq, RWKV_WIDTH)
    return y * g


def hmoe(h, rg_w, rg_b, re_w, re_b, w1, w3, w2):
    bsz, seq, d = h.shape
    hf = h.reshape(-1, d)
    n_tok = hf.shape[0]
    pg = jax.nn.softmax((hf @ rg_w + rg_b).astype(jnp.float32), axis=-1)
    gsel = jnp.argmax(pg, axis=-1)
    gw = jnp.max(pg, axis=-1, keepdims=True)
    le = (hf @ re_w + re_b).astype(jnp.float32).reshape(n_tok, N_GROUPS, EXPERTS_PER_GROUP)
    le = jnp.take_along_axis(le, gsel[:, None, None], axis=1)[:, 0]
    top_p, top_i = lax.top_k(jax.nn.softmax(le, axis=-1), TOP_K_IN_GROUP)
    top_p = top_p / jnp.sum(top_p, axis=-1, keepdims=True)
    eid = (gsel[:, None] * EXPERTS_PER_GROUP + top_i).reshape(-1)
    wts = (gw * top_p).reshape(-1)
    tok = jnp.repeat(jnp.arange(n_tok, dtype=jnp.int32), TOP_K_IN_GROUP)
    n_slot = eid.shape[0]
    order = jnp.argsort(eid)
    se = eid[order]
    counts = jnp.bincount(eid, length=N_EXPERTS)
    start = jnp.cumsum(counts) - counts
    pcounts = (counts + MOE_BLOCK - 1) // MOE_BLOCK * MOE_BLOCK
    pend = jnp.cumsum(pcounts)
    pstart = pend - pcounts
    pos = pstart[se] + jnp.arange(n_slot) - start[se]
    n_blocks = -(-n_slot // MOE_BLOCK) + N_EXPERTS
    buf_tok = jnp.zeros((n_blocks * MOE_BLOCK,), jnp.int32).at[pos].set(tok[order])
    buf_w = jnp.zeros((n_blocks * MOE_BLOCK,), jnp.float32).at[pos].set(wts[order])
    block_exp = jnp.minimum(jnp.searchsorted(pend, jnp.arange(n_blocks) * MOE_BLOCK, side='right'), N_EXPERTS - 1)

    def block_step(args):
        t, e, wt = args
        xb = hf[t]
        hid = jax.nn.silu(xb @ w1[e]) * (xb @ w3[e])
        return (hid @ w2[e]) * wt[:, None].astype(hf.dtype)

    yb = lax.map(block_step, (buf_tok.reshape(n_blocks, MOE_BLOCK), block_exp, buf_w.reshape(n_blocks, MOE_BLOCK)))
    out = jnp.zeros_like(hf).at[buf_tok].add(yb.reshape(-1, d))
    return out.reshape(bsz, seq, d)


def encoder_layer(x, c, ada_w, ada_b, norm1_g, norm2_g, w_in, na_rpb, rwkv_mu, rwkv_w0, rwkv_w2,
                  rwkv_a0, rwkv_a2, rwkv_g2, rwkv_kk, rwkv_ka, rwkv_rk, rwkv_lnx_g, rwkv_lnx_b, w_out,
                  router_g_w, router_g_b, router_e_w, router_e_b, exp_w1, exp_w3, exp_w2):
    mod = (jax.nn.silu(c) @ ada_w + ada_b)[:, None, :]
    sh1, sc1, g1, sh2, sc2, g2 = jnp.split(mod, 6, axis=-1)
    h = rms_norm(x, norm1_g) * (1 + sc1) + sh1
    z = h @ w_in
    q_a, k_a, v_a, z_r = jnp.split(z, [NA_WIDTH, 2 * NA_WIDTH, 3 * NA_WIDTH], axis=-1)
    att = neighbourhood_attention(q_a, k_a, v_a, na_rpb)
    rec = rwkv7_bidirectional(z_r, rwkv_mu, rwkv_w0, rwkv_w2, rwkv_a0, rwkv_a2, rwkv_g2,
                              rwkv_kk, rwkv_ka, rwkv_rk, rwkv_lnx_g, rwkv_lnx_b).astype(x.dtype)
    x = x + g1 * (jnp.concatenate([att, rec], axis=-1) @ w_out)
    h = rms_norm(x, norm2_g) * (1 + sc2) + sh2
    x = x + g2 * hmoe(h, router_g_w, router_g_b, router_e_w, router_e_b, exp_w1, exp_w3, exp_w2)
    return x


def setup_inputs(seed: int = 0) -> dict:
    key = jax.random.key(seed)
    ks = jax.random.split(key, 32)
    nrm = lambda i, shape, s: jax.random.normal(ks[i], shape, jnp.float32) * s
    L, D = DEPTH, D_MODEL
    return {
        'x_prompt': nrm(0, (BATCH, SEQ, D), 1.0),
        'x_sample': nrm(1, (DEC_BATCH, DEC_SEQ, D), 1.0),
        'c_prompt': nrm(2, (BATCH, D), 1.0),
        'c_sample': nrm(3, (DEC_BATCH, D), 1.0),
        'ada_w': nrm(4, (L, D, 6 * D), 0.5 * D ** -0.5),
        'ada_b': nrm(5, (L, 6 * D), 0.02),
        'norm1_g': 1.0 + nrm(6, (L, D), 0.05),
        'norm2_g': 1.0 + nrm(7, (L, D), 0.05),
        'w_in': nrm(8, (L, D, IN_WIDTH), D ** -0.5),
        'na_rpb': nrm(9, (L, NA_HEADS, 2 * NA_KH - 1, 2 * NA_KW - 1), 0.1),
        'rwkv_mu': jax.random.uniform(ks[10], (L, 2, RWKV_IN), jnp.float32, 0.0, 0.5),
        'rwkv_w0': jax.random.uniform(ks[11], (L, 2, RWKV_WIDTH), jnp.float32, -6.5, -1.5),
        'rwkv_w2': nrm(12, (L, 2, DECAY_LORA, RWKV_WIDTH), 0.1 * DECAY_LORA ** -0.5),
        'rwkv_a0': nrm(13, (L, 2, RWKV_WIDTH), 0.1),
        'rwkv_a2': nrm(14, (L, 2, ICLR_LORA, RWKV_WIDTH), 0.1 * ICLR_LORA ** -0.5),
        'rwkv_g2': nrm(15, (L, GATE_LORA, RWKV_WIDTH), GATE_LORA ** -0.5),
        'rwkv_kk': 0.85 + nrm(16, (L, RWKV_WIDTH), 0.05),
        'rwkv_ka': 1.0 + nrm(17, (L, RWKV_WIDTH), 0.05),
        'rwkv_rk': nrm(18, (L, RWKV_HEADS, RWKV_HEAD_DIM), 0.1),
        'rwkv_lnx_g': 1.0 + nrm(19, (L, RWKV_WIDTH), 0.05),
        'rwkv_lnx_b': nrm(20, (L, RWKV_WIDTH), 0.02),
        'w_out': nrm(21, (L, D, D), D ** -0.5),
        'router_g_w': nrm(22, (L, D, N_GROUPS), D ** -0.5),
        'router_g_b': nrm(23, (L, N_GROUPS), 0.01),
        'router_e_w': nrm(24, (L, D, N_EXPERTS), D ** -0.5),
        'router_e_b': nrm(25, (L, N_EXPERTS), 0.01),
        'exp_w1': nrm(26, (L, N_EXPERTS, D, EXPERT_HIDDEN), D ** -0.5),
        'exp_w3': nrm(27, (L, N_EXPERTS, D, EXPERT_HIDDEN), D ** -0.5),
        'exp_w2': nrm(28, (L, N_EXPERTS, EXPERT_HIDDEN, D), EXPERT_HIDDEN ** -0.5),
        'final_g': 1.0 + nrm(29, (D,), 0.05),
    }


def reference(x_prompt, x_sample, c_prompt, c_sample, ada_w, ada_b, norm1_g, norm2_g, w_in, na_rpb,
              rwkv_mu, rwkv_w0, rwkv_w2, rwkv_a0, rwkv_a2, rwkv_g2, rwkv_kk, rwkv_ka, rwkv_rk,
              rwkv_lnx_g, rwkv_lnx_b, w_out, router_g_w, router_g_b, router_e_w, router_e_b,
              exp_w1, exp_w3, exp_w2, final_g):
    def trunk(x, c):
        for l in range(DEPTH):
            x = encoder_layer(x, c, ada_w[l], ada_b[l], norm1_g[l], norm2_g[l], w_in[l], na_rpb[l],
                              rwkv_mu[l], rwkv_w0[l], rwkv_w2[l], rwkv_a0[l], rwkv_a2[l], rwkv_g2[l],
                              rwkv_kk[l], rwkv_ka[l], rwkv_rk[l], rwkv_lnx_g[l], rwkv_lnx_b[l], w_out[l],
                              router_g_w[l], router_g_b[l], router_e_w[l], router_e_b[l],
                              exp_w1[l], exp_w3[l], exp_w2[l])
        return rms_norm(x, final_g)

    y_prompt = trunk(x_prompt, c_prompt)
    y_sample = trunk(x_sample, c_sample)
    return (y_prompt, y_sample)
```

```python
import functools

import numpy as np
import jax
import jax.numpy as jnp
from jax import lax
from jax.experimental import pallas as pl
from jax.experimental.pallas import tpu as pltpu

F32 = jnp.float32
BF16 = jnp.bfloat16

D_MODEL = 1024
GRID_W = 64
N_HEADS = 8
HEAD_DIM = 64
MIX_W = N_HEADS * HEAD_DIM
NA_KH = 8
NA_KW = 16
RWKV_IN = 3 * MIX_W + 4 * 64 + 128
IN_WIDTH = 3 * MIX_W + RWKV_IN
N_GROUPS = 4
EXPERTS_PER_GROUP = 8
N_EXPERTS = N_GROUPS * EXPERTS_PER_GROUP
EXPERT_HIDDEN = 512
NORM_EPS = 1e-6
LNX_EPS = 64e-5
MASK_VALUE = -1e30

LANES = 128
N_SLABS = MIX_W // LANES
CHUNK = 64
VMEM_LIMIT = 56 * 1024 * 1024

TM_PROJ = 256
TP_PREP = 256
SCAN_ROWS = 256
TM_MOE = 1024
MOE_BLK = 128
ROUTER_ROWS = 40


def _bf(x):
    return x.astype(BF16)


def _dot(a, b):
    return jnp.dot(a, b, preferred_element_type=F32)


def _dot_nt(a, b):
    return lax.dot_general(a, b, (((1,), (1,)), ((), ())), preferred_element_type=F32)


def _dot_tn(a, b):
    return lax.dot_general(a, b, (((0,), (0,)), ((), ())), preferred_element_type=F32)


def _split(x):
    hi = _bf(x)
    return hi, _bf(x - hi.astype(F32))


def _dot_hilo(x, m_bf):
    hi, lo = _split(x)
    return _dot(hi, m_bf) + _dot(lo, m_bf)


def _ldot_hilo(m_bf, x):
    hi, lo = _split(x)
    return _dot(m_bf, hi) + _dot(m_bf, lo)


def _params(*sem):
    return pltpu.CompilerParams(dimension_semantics=sem, vmem_limit_bytes=VMEM_LIMIT)


def _full(shape):
    n = len(shape)
    return pl.BlockSpec(shape, lambda *_: (0,) * n)


def _mod_kernel(c_ref, w_ref, b_ref, o_ref):
    c = c_ref[...]
    s = c * jax.nn.sigmoid(c)
    o_ref[...] = jnp.dot(s, w_ref[...], precision=lax.Precision.HIGHEST,
                         preferred_element_type=F32) + b_ref[...]


def _modulation(c, ada_w, ada_b):
    nb = c.shape[0]
    tn = 1024
    return pl.pallas_call(
        _mod_kernel,
        grid=(6 * D_MODEL // tn,),
        in_specs=[_full((nb, D_MODEL)),
                  pl.BlockSpec((D_MODEL, tn), lambda j: (0, j)),
                  pl.BlockSpec((1, tn), lambda j: (0, j))],
        out_specs=pl.BlockSpec((nb, tn), lambda j: (0, j)),
        out_shape=jax.ShapeDtypeStruct((nb, 6 * D_MODEL), F32),
        compiler_params=_params("arbitrary"),
        name="mod",
    )(c, ada_w, ada_b.reshape(1, -1))


def _rms(x):
    return x * lax.rsqrt(jnp.mean(x * x, axis=-1, keepdims=True) + NORM_EPS)


def _inproj_kernel(x_ref, mod_ref, g_ref, w_ref, q_ref, k_ref, v_ref, zr_ref):
    h = _rms(x_ref[0]) * g_ref[...]
    h = h * (1.0 + mod_ref[0, 1:2, :]) + mod_ref[0, 0:1, :]
    z = _dot(_bf(h), w_ref[...])
    q_ref[0] = _bf(z[:, 0:MIX_W] * (HEAD_DIM ** -0.5))
    k_ref[0] = _bf(z[:, MIX_W:2 * MIX_W])
    v_ref[0] = _bf(z[:, 2 * MIX_W:3 * MIX_W])
    zr_ref[0] = z[:, 3 * MIX_W:]


def _inproj(x, mod, norm_g, w_in_bf):
    b, t, d = x.shape
    tm = TM_PROJ
    tok = lambda w: pl.BlockSpec((1, tm, w), lambda i, j: (i, j, 0))
    return pl.pallas_call(
        _inproj_kernel,
        grid=(b, t // tm),
        in_specs=[tok(d),
                  pl.BlockSpec((1, 6, d), lambda i, j: (i, 0, 0)),
                  _full((1, d)),
                  _full((d, IN_WIDTH))],
        out_specs=[tok(MIX_W), tok(MIX_W), tok(MIX_W), tok(RWKV_IN)],
        out_shape=[jax.ShapeDtypeStruct((b, t, MIX_W), BF16)] * 3
        + [jax.ShapeDtypeStruct((b, t, RWKV_IN), F32)],
        compiler_params=_params("parallel", "arbitrary"),
        name="inproj",
    )(x, mod, norm_g.reshape(1, d), w_in_bf)


def _na_bias_table(rpb):
    qc = np.arange(GRID_W)[:, None]
    kc = np.arange(GRID_W)[None, :]
    win = np.clip(qc - NA_KW // 2, 0, GRID_W - NA_KW)
    valid = (kc >= win) & (kc < win + NA_KW)
    dx = np.clip(kc - qc + NA_KW - 1, 0, 2 * NA_KW - 2)
    dy = np.arange(NA_KH)[:, None] + np.arange(NA_KH)[None, :]
    tab = rpb.astype(F32)[:, dy][:, :, :, dx]
    tab = jnp.where(jnp.asarray(valid)[None, None, None], tab, MASK_VALUE)
    tab = jnp.transpose(tab, (0, 1, 3, 2, 4))
    return tab.reshape(N_HEADS, NA_KH, GRID_W, NA_KH * GRID_W)


def _na_kernel(q_ref, k_ref, v_ref, bias_ref, o_ref, *, rows):
    lane = lax.broadcasted_iota(jnp.int32, (GRID_W, LANES), 1)
    first = lane < HEAD_DIM
    nk = NA_KH * GRID_W

    def row_body(i, carry):
        rs = jnp.clip(i - NA_KH // 2, 0, rows - NA_KH)
        d0 = rs - i + (NA_KH - 1)
        q0 = pl.multiple_of(i * GRID_W, GRID_W)
        k0 = pl.multiple_of(rs * GRID_W, GRID_W)
        for p in range(N_SLABS):
            sl = slice(p * LANES, (p + 1) * LANES)
            q2 = q_ref[0, pl.ds(q0, GRID_W), sl].astype(F32)
            k8 = k_ref[0, pl.ds(k0, nk), sl]
            v8 = v_ref[0, pl.ds(k0, nk), sl]
            outs = []
            for hh in range(2):
                qm = _bf(jnp.where(first if hh == 0 else ~first, q2, 0.0))
                s = _dot_nt(qm, k8) + bias_ref[2 * p + hh, d0]
                e = jnp.exp(s - jnp.max(s, axis=-1, keepdims=True))
                l = jnp.sum(e, axis=-1, keepdims=True)
                outs.append(_dot(_bf(e), v8) / l)
            o_ref[0, pl.ds(q0, GRID_W), sl] = _bf(jnp.where(first, outs[0], outs[1]))
        return carry

    lax.fori_loop(0, rows, row_body, 0)


def _neighbourhood_attention(q, k, v, bias):
    b, t, w = q.shape
    rows = t // GRID_W
    assert rows >= NA_KH
    seq = pl.BlockSpec((1, t, w), lambda i: (i, 0, 0))
    return pl.pallas_call(
        functools.partial(_na_kernel, rows=rows),
        grid=(b,),
        in_specs=[seq, seq, seq, pl.BlockSpec(memory_space=pltpu.VMEM)],
        out_specs=seq,
        out_shape=jax.ShapeDtypeStruct((b, t, w), BF16),
        compiler_params=_params("parallel"),
        name="na",
    )(q, k, v, bias)


def _prep_kernel(z_ref, zp_ref, zn_ref, mu_ref, w0_ref, w2_ref, a0_ref, a2_ref, g2_ref,
                 kk_ref, ka_ref, rk_ref, ho_ref, trif_ref, trib_ref, ones_ref,
                 rhf_ref, ahf_ref, bhf_ref, khf_ref, btf_ref, ktf_ref,
                 rhb_ref, ahb_ref, bhb_ref, khb_ref, btb_ref, ktb_ref,
                 v_ref, bonus_ref, g_ref, wc_ref, *, n_tiles):
    j = pl.program_id(1)
    tp = z_ref.shape[1]
    row = lax.broadcasted_iota(jnp.int32, (tp, 1), 0)
    prev_row = jnp.where(j > 0, zp_ref[0, 7:8, :], 0.0)
    next_row = jnp.where(j < n_tiles - 1, zn_ref[0, 0:1, :], 0.0)

    def shifted(c0, c1):
        zc = z_ref[0, :, c0:c1]
        prev = jnp.where(row == 0, prev_row[:, c0:c1], pltpu.roll(zc, 1, 0))
        nxt = jnp.where(row == tp - 1, next_row[:, c0:c1], pltpu.roll(zc, tp - 1, 0))
        return zc + mu_ref[0:1, c0:c1] * (prev - zc) + mu_ref[1:2, c0:c1] * (nxt - zc)

    r = shifted(0, MIX_W)
    k = shifted(MIX_W, 2 * MIX_W)
    v = shifted(2 * MIX_W, 3 * MIX_W)
    xw = shifted(3 * MIX_W, 3 * MIX_W + LANES)
    xa = shifted(3 * MIX_W + LANES, 3 * MIX_W + 2 * LANES)
    xg = shifted(3 * MIX_W + 2 * LANES, RWKV_IN)
    ho = ho_ref[...]

    g_ref[0] = _dot(_bf(jax.nn.sigmoid(xg)), g2_ref[...])
    v_ref[0] = _bf(v)
    kk = k * kk_ref[...]
    kk = kk / jnp.maximum(jnp.sqrt(_dot_hilo(kk * kk, ho)), 1e-12)
    tw = _bf(jnp.tanh(xw))
    xab = _bf(xa)

    kd_sum = jnp.zeros_like(k)
    dirs = ((trif_ref, (rhf_ref, ahf_ref, bhf_ref, khf_ref, btf_ref, ktf_ref)),
            (trib_ref, (rhb_ref, ahb_ref, bhb_ref, khb_ref, btb_ref, ktb_ref)))
    for d, (tri_ref, outs) in enumerate(dirs):
        y = w0_ref[d:d + 1, :] + _dot(tw, w2_ref[d])
        sp = jnp.maximum(-y, 0.0) + jnp.log1p(jnp.exp(-jnp.abs(y)))
        ld = -jnp.exp(-sp - 0.5)
        a = jax.nn.sigmoid(a0_ref[d:d + 1, :] + _dot(xab, a2_ref[d]))
        kd = k * (1.0 + (a - 1.0) * ka_ref[...])
        kd_sum = kd_sum + kd
        cum = _ldot_hilo(tri_ref[...], ld)
        tot = _ldot_hilo(ones_ref[...], ld)
        w_inv = jnp.exp(-cum)
        w_rest = jnp.exp(tot - cum)
        b = kk * a
        outs[0][0] = _bf(r * jnp.exp(cum))
        outs[1][0] = _bf(-kk * jnp.exp(cum - ld))
        outs[2][0] = _bf(b * w_inv)
        outs[3][0] = _bf(kd * w_inv)
        outs[4][0] = _bf(b * w_rest)
        outs[5][0] = _bf(kd * w_rest)
        for c in range(tp // CHUNK):
            wc_ref[0, c, d:d + 1, :] = jnp.exp(tot[c * CHUNK:c * CHUNK + 1, :])
    bonus_ref[0] = _dot_hilo(r * kd_sum * rk_ref[...], ho) * v


def _chunk_matrices(tp):
    t = np.arange(tp)
    same = (t[:, None] // CHUNK) == (t[None, :] // CHUNK)
    trif = same & (t[None, :] <= t[:, None])
    trib = same & (t[None, :] >= t[:, None])
    return [jnp.asarray(m, BF16) for m in (trif, trib, same)]


def _head_ones():
    h = np.arange(MIX_W) // HEAD_DIM
    return jnp.asarray(h[:, None] == h[None, :], BF16)


def _rwkv_prep(zr, mu, w0, w2, a0, a2, g2, k_k, k_a, r_k):
    b, t, _ = zr.shape
    tp = TP_PREP
    n_tiles = t // tp
    nc = t // CHUNK
    zpad = jnp.zeros((2, 64, MIX_W), F32)
    w2p = _bf(jnp.stack([jnp.concatenate([w2[0], zpad[0]], 0), jnp.concatenate([zpad[0], w2[1]], 0)]))
    a2p = _bf(jnp.stack([jnp.concatenate([a2[0], zpad[0]], 0), jnp.concatenate([zpad[0], a2[1]], 0)]))
    trif, trib, ones = _chunk_matrices(tp)
    hb = tp // 8
    tok = lambda w: pl.BlockSpec((1, tp, w), lambda i, j: (i, j, 0))
    row = lambda a: a.reshape(1, MIX_W)
    outs = pl.pallas_call(
        functools.partial(_prep_kernel, n_tiles=n_tiles),
        grid=(b, n_tiles),
        in_specs=[tok(RWKV_IN),
                  pl.BlockSpec((1, 8, RWKV_IN), lambda i, j: (i, jnp.maximum(j * hb - 1, 0), 0)),
                  pl.BlockSpec((1, 8, RWKV_IN), lambda i, j: (i, jnp.minimum((j + 1) * hb, t // 8 - 1), 0)),
                  _full((2, RWKV_IN)), _full((2, MIX_W)), _full((2, LANES, MIX_W)),
                  _full((2, MIX_W)), _full((2, LANES, MIX_W)), _full((LANES, MIX_W)),
                  _full((1, MIX_W)), _full((1, MIX_W)), _full((1, MIX_W)),
                  _full((MIX_W, MIX_W)), _full((tp, tp)), _full((tp, tp)), _full((tp, tp))],
        out_specs=[tok(MIX_W)] * 13 + [tok(MIX_W), tok(MIX_W),
                   pl.BlockSpec((1, tp // CHUNK, 2, MIX_W), lambda i, j: (i, j, 0, 0))],
        out_shape=[jax.ShapeDtypeStruct((b, t, MIX_W), BF16)] * 13
        + [jax.ShapeDtypeStruct((b, t, MIX_W), F32)] * 2
        + [jax.ShapeDtypeStruct((b, nc, 2, MIX_W), F32)],
        compiler_params=_params("parallel", "arbitrary"),
        name="rwkv_prep",
    )(zr, zr, zr, mu, w0, w2p, a0, a2p, _bf(g2), row(k_k), row(k_a), row(r_k),
      _head_ones(), trif, trib, ones)
    return outs


N_INV_LEVELS = CHUNK.bit_length() - 1


def _inverse_level_masks():
    idx = np.arange(LANES)
    t = idx % CHUNK
    same_head = (idx[:, None] // CHUNK) == (idx[None, :] // CHUNK)
    out = np.zeros((2, N_INV_LEVELS, LANES, LANES), np.float32)
    for lvl in range(N_INV_LEVELS):
        s = 1 << lvl
        same_blk = (t[:, None] // (2 * s)) == (t[None, :] // (2 * s))
        hi_r, hi_c = (t[:, None] % (2 * s)) >= s, (t[None, :] % (2 * s)) >= s
        out[0, lvl] = same_head & same_blk & hi_r & ~hi_c
        out[1, lvl] = same_head & same_blk & ~hi_r & hi_c
    return jnp.asarray(out, BF16)


def _scan_kernel(*refs, n_sub):
    fwd_in, bwd_in = refs[0:8], refs[8:16]
    lm_ref, yf_ref, yb_ref, st_ref = refs[16], refs[17], refs[18], refs[19]
    j = pl.program_id(1)
    eye = _bf((lax.broadcasted_iota(jnp.int32, (LANES, LANES), 0)
               == lax.broadcasted_iota(jnp.int32, (LANES, LANES), 1)).astype(F32))

    @pl.when(j == 0)
    def _():
        st_ref[...] = jnp.zeros_like(st_ref)

    lane = lax.broadcasted_iota(jnp.int32, (CHUNK, LANES), 1)
    m_a = _bf((lane < HEAD_DIM).astype(F32))
    m_b = _bf((lane >= HEAD_DIM).astype(F32))
    ti = lax.broadcasted_iota(jnp.int32, (LANES, LANES), 0) % CHUNK
    tj = lax.broadcasted_iota(jnp.int32, (LANES, LANES), 1) % CHUNK

    def stack(z):
        return jnp.concatenate([z * m_a, z * m_b], axis=0)

    def one_chunk(d, ins, y_ref, c):
        rh, ah, bh, kh, bt, kt, vv, wc = ins
        strict = (ti > tj) if d == 0 else (ti < tj)
        incl = (ti >= tj) if d == 0 else (ti <= tj)
        r0 = pl.multiple_of(c * CHUNK, CHUNK)
        for p in range(N_SLABS):
            sl = slice(p * LANES, (p + 1) * LANES)
            ld = lambda ref: stack(ref[0, pl.ds(r0, CHUNK), sl])
            a_s, r_s, b_s, k_s, bt_s, kt_s, v_s = (ld(ah), ld(rh), ld(bh), ld(kh), ld(bt), ld(kt), ld(vv))
            q2 = st_ref[d, p]
            lhs = jnp.concatenate([a_s, r_s], axis=0)
            lp = _dot_nt(lhs, _bf(q2))
            aa = _dot_nt(lhs, jnp.concatenate([b_s, k_s], axis=0))
            n_ab = jnp.where(strict, aa[0:LANES, 0:LANES], 0.0)
            n_ak = jnp.where(strict, aa[0:LANES, LANES:], 0.0)
            n_rb = jnp.where(incl, aa[LANES:, 0:LANES], 0.0)
            n_rk = jnp.where(incl, aa[LANES:, LANES:], 0.0)
            x = lp[0:LANES] + _dot(_bf(n_ak), v_s)
            nb = _bf(n_ab)
            t_inv = eye + nb * lm_ref[d, 0]
            for lvl in range(1, N_INV_LEVELS):
                de = _bf(_dot(t_inv, nb * lm_ref[d, lvl]))
                t_inv = t_inv + _bf(_dot(de, t_inv))
            u = _dot(t_inv, _bf(x))
            uv = jnp.concatenate([_bf(u), v_s], axis=0)
            ys = lp[LANES:] + _dot(_bf(jnp.concatenate([n_rb, n_rk], axis=1)), uv)
            y_ref[0, pl.ds(r0, CHUNK), sl] = ys[0:CHUNK] + ys[CHUNK:]
            wrow = wc[0, pl.ds(c, 1), d, sl]
            st_ref[d, p] = q2 * wrow + _dot_tn(uv, jnp.concatenate([bt_s, kt_s], axis=0))

    def sub(c, carry):
        one_chunk(0, fwd_in, yf_ref, c)
        one_chunk(1, bwd_in, yb_ref, n_sub - 1 - c)
        return carry

    lax.fori_loop(0, n_sub, sub, 0)


def _rwkv_scan(prep):
    (rhf, ahf, bhf, khf, btf, ktf, rhb, ahb, bhb, khb, btb, ktb, v, _, _, wc) = prep
    b, t, _ = v.shape
    rows = SCAN_ROWS
    n_sub = rows // CHUNK
    ns = t // rows
    f_tok = pl.BlockSpec((1, rows, MIX_W), lambda i, j: (i, j, 0))
    b_tok = pl.BlockSpec((1, rows, MIX_W), lambda i, j: (i, ns - 1 - j, 0))
    f_wc = pl.BlockSpec((1, n_sub, 2, MIX_W), lambda i, j: (i, j, 0, 0))
    b_wc = pl.BlockSpec((1, n_sub, 2, MIX_W), lambda i, j: (i, ns - 1 - j, 0, 0))
    return pl.pallas_call(
        functools.partial(_scan_kernel, n_sub=n_sub),
        grid=(b, ns),
        in_specs=[f_tok] * 7 + [f_wc] + [b_tok] * 7 + [b_wc] + [_full((2, N_INV_LEVELS, LANES, LANES))],
        out_specs=[f_tok, b_tok],
        out_shape=[jax.ShapeDtypeStruct((b, t, MIX_W), F32)] * 2,
        scratch_shapes=[pltpu.VMEM((2, N_SLABS, LANES, LANES), F32)],
        compiler_params=_params("parallel", "arbitrary"),
        name="rwkv_scan",
    )(rhf, ahf, bhf, khf, btf, ktf, v, wc, rhb, ahb, bhb, khb, btb, ktb, v, wc, _inverse_level_masks())


def _outproj_kernel(att_ref, yf_ref, yb_ref, bonus_ref, g_ref, x_ref, mod_ref, wa_ref, wr_ref,
                    lg_ref, lb_ref, n2_ref, hm_ref, x1_ref, h2_ref):
    wkv = yf_ref[0] + yb_ref[0]
    hm = hm_ref[...]
    mean = _dot_hilo(wkv, hm) * (1.0 / HEAD_DIM)
    dev = wkv - mean
    var = _dot_hilo(dev * dev, hm) * (1.0 / HEAD_DIM)
    rec = dev * lax.rsqrt(var + LNX_EPS) * lg_ref[...] + lb_ref[...]
    rec = (rec + bonus_ref[0]) * g_ref[0]
    o = _dot(att_ref[0], wa_ref[...]) + _dot(_bf(rec), wr_ref[...])
    x1 = x_ref[0] + mod_ref[0, 2:3, :] * o
    x1_ref[0] = x1
    h = _rms(x1) * n2_ref[...]
    h2_ref[0] = _bf(h * (1.0 + mod_ref[0, 4:5, :]) + mod_ref[0, 3:4, :])


def _outproj(att, yf, yb, bonus, g, x, mod, w_out_bf, lnx_g, lnx_b, norm2_g):
    b, t, d = x.shape
    tm = TM_PROJ
    tok = lambda w: pl.BlockSpec((1, tm, w), lambda i, j: (i, j, 0))
    return pl.pallas_call(
        _outproj_kernel,
        grid=(b, t // tm),
        in_specs=[tok(MIX_W)] * 5 + [tok(d), pl.BlockSpec((1, 6, d), lambda i, j: (i, 0, 0)),
                  _full((MIX_W, d)), _full((MIX_W, d)), _full((1, MIX_W)), _full((1, MIX_W)),
                  _full((1, d)), _full((MIX_W, MIX_W))],
        out_specs=[tok(d), tok(d)],
        out_shape=[jax.ShapeDtypeStruct((b, t, d), F32), jax.ShapeDtypeStruct((b, t, d), BF16)],
        compiler_params=_params("parallel", "arbitrary"),
        name="outproj",
    )(att, yf, yb, bonus, g, x, mod, w_out_bf[:MIX_W], w_out_bf[MIX_W:],
      lnx_g.reshape(1, MIX_W), lnx_b.reshape(1, MIX_W), norm2_g.reshape(1, d), _head_ones())


def _first_index_of_max(x, m, n):
    idx = lax.broadcasted_iota(jnp.int32, x.shape, 0)
    return jnp.min(jnp.where(x == m, idx, n), axis=0, keepdims=True)


def _route(logit_t):
    lg = logit_t[0:N_GROUPS]
    mg = jnp.max(lg, axis=0, keepdims=True)
    pg = jnp.exp(lg - mg)
    pg = pg / jnp.sum(pg, axis=0, keepdims=True)
    gw = jnp.max(pg, axis=0, keepdims=True)
    gsel = _first_index_of_max(lg, mg, N_GROUPS)
    le = jnp.zeros((EXPERTS_PER_GROUP, logit_t.shape[1]), F32)
    for gi in range(N_GROUPS):
        rows = logit_t[8 + EXPERTS_PER_GROUP * gi:8 + EXPERTS_PER_GROUP * (gi + 1)]
        le = jnp.where(gsel == gi, rows, le)
    me = jnp.max(le, axis=0, keepdims=True)
    pe = jnp.exp(le - me)
    pe = pe / jnp.sum(pe, axis=0, keepdims=True)
    p1 = jnp.max(pe, axis=0, keepdims=True)
    i1 = _first_index_of_max(pe, p1, EXPERTS_PER_GROUP)
    idx = lax.broadcasted_iota(jnp.int32, pe.shape, 0)
    rest = jnp.where(idx == i1, -1.0, pe)
    p2 = jnp.max(rest, axis=0, keepdims=True)
    i2 = _first_index_of_max(rest, p2, EXPERTS_PER_GROUP)
    ps = p1 + p2
    e1 = gsel * EXPERTS_PER_GROUP + i1
    e2 = gsel * EXPERTS_PER_GROUP + i2
    return e1, e2, gw * (p1 / ps), gw * (p2 / ps)


def _moe_kernel(h2_ref, x1_ref, mod_ref, wrh_ref, wrl_ref, br_ref, tri_ref, fg_ref,
                w1_ref, w3_ref, w2_ref, o_ref, key_ref, wx_ref):
    e = pl.program_id(2)
    tm = h2_ref.shape[1]

    @pl.when(e == 0)
    def _():
        h2 = h2_ref[0]
        logit_t = _dot_nt(wrh_ref[...], h2) + _dot_nt(wrl_ref[...], h2) + br_ref[:, 0:1]
        e1, e2, w1, w2 = _route(logit_t)
        eid = lax.broadcasted_iota(jnp.int32, (N_EXPERTS, tm), 0)
        hit1 = eid == e1
        hit2 = eid == e2
        sel = jnp.where(hit1 | hit2, 1.0, 0.0)
        rank = _dot(_bf(sel), tri_ref[...])
        key_ref[...] = jnp.where(sel > 0.0, rank, -1.0).astype(jnp.int32)
        wx_ref[...] = jnp.where(hit1, w1, jnp.where(hit2, w2, 0.0))
        o_ref[0] = jnp.zeros((tm, D_MODEL), F32)

    key = key_ref[pl.ds(e, 1), :]
    wrow = wx_ref[pl.ds(e, 1), :]
    count = jnp.max(key) + 1
    slot0 = lax.broadcasted_iota(jnp.int32, (MOE_BLK, 1), 0)

    def block(jb, carry):
        hit = key == (slot0 + jb * MOE_BLK)
        gsel = _bf(jnp.where(hit, 1.0, 0.0))
        xb = _bf(_dot(gsel, h2_ref[0]))
        hid = jax.nn.silu(_dot(xb, w1_ref[0])) * _dot(xb, w3_ref[0])
        yb = _dot(_bf(hid), w2_ref[0])
        ws = jnp.sum(jnp.where(hit, wrow, 0.0), axis=1, keepdims=True)
        o_ref[0] += _dot_tn(gsel, _bf(yb * ws))
        return carry

    lax.fori_loop(0, (count + MOE_BLK - 1) // MOE_BLK, block, 0)

    @pl.when(e == N_EXPERTS - 1)
    def _():
        x2 = x1_ref[0] + mod_ref[0, 5:6, :] * o_ref[0]
        o_ref[0] = _rms(x2) * fg_ref[...]


def _moe(h2, x1, mod, rg_w, rg_b, re_w, re_b, w1_bf, w3_bf, w2_bf, final_g):
    b, t, d = x1.shape
    tm = TM_MOE
    wr = jnp.zeros((ROUTER_ROWS, d), F32).at[0:N_GROUPS].set(rg_w.T).at[8:].set(re_w.T)
    wr_hi = _bf(wr)
    wr_lo = _bf(wr - wr_hi.astype(F32))
    br = jnp.zeros((ROUTER_ROWS,), F32).at[0:N_GROUPS].set(rg_b).at[8:].set(re_b)
    br = jnp.broadcast_to(br[:, None], (ROUTER_ROWS, LANES))
    tt = np.arange(tm)
    tri = jnp.asarray(tt[:, None] < tt[None, :], BF16)
    tok = lambda w: pl.BlockSpec((1, tm, w), lambda i, j, e: (i, j, 0))
    return pl.pallas_call(
        _moe_kernel,
        grid=(b, t // tm, N_EXPERTS),
        in_specs=[tok(d), tok(d), pl.BlockSpec((1, 6, d), lambda i, j, e: (i, 0, 0)),
                  _full((ROUTER_ROWS, d)), _full((ROUTER_ROWS, d)), _full((ROUTER_ROWS, LANES)),
                  _full((tm, tm)), _full((1, d)),
                  pl.BlockSpec((1, d, EXPERT_HIDDEN), lambda i, j, e: (e, 0, 0)),
                  pl.BlockSpec((1, d, EXPERT_HIDDEN), lambda i, j, e: (e, 0, 0)),
                  pl.BlockSpec((1, EXPERT_HIDDEN, d), lambda i, j, e: (e, 0, 0))],
        out_specs=tok(d),
        out_shape=jax.ShapeDtypeStruct((b, t, d), F32),
        scratch_shapes=[pltpu.VMEM((N_EXPERTS, tm), jnp.int32), pltpu.VMEM((N_EXPERTS, tm), F32)],
        compiler_params=_params("parallel", "arbitrary", "arbitrary"),
        name="moe",
    )(h2, x1, mod, wr_hi, wr_lo, br, tri, final_g.reshape(1, d), w1_bf, w3_bf, w2_bf)


def _trunk(x, mod, p):
    q, k, v, zr = _inproj(x, mod, p["norm1_g"], p["w_in"])
    att = _neighbourhood_attention(q, k, v, p["na_bias"])
    prep = _rwkv_prep(zr, p["rwkv_mu"], p["rwkv_w0"], p["rwkv_w2"], p["rwkv_a0"], p["rwkv_a2"],
                      p["rwkv_g2"], p["rwkv_kk"], p["rwkv_ka"], p["rwkv_rk"])
    yf, yb = _rwkv_scan(prep)
    x1, h2 = _outproj(att, yf, yb, prep[13], prep[14], x, mod, p["w_out"],
                      p["rwkv_lnx_g"], p["rwkv_lnx_b"], p["norm2_g"])
    return _moe(h2, x1, mod, p["router_g_w"], p["router_g_b"], p["router_e_w"], p["router_e_b"],
                p["exp_w1"], p["exp_w3"], p["exp_w2"], p["final_g"])


def kernel(x_prompt, x_sample, c_prompt, c_sample, ada_w, ada_b, norm1_g, norm2_g, w_in, na_rpb, rwkv_mu, rwkv_w0, rwkv_w2, rwkv_a0, rwkv_a2, rwkv_g2, rwkv_kk, rwkv_ka, rwkv_rk, rwkv_lnx_g, rwkv_lnx_b, w_out, router_g_w, router_g_b, router_e_w, router_e_b, exp_w1, exp_w3, exp_w2, final_g):
    assert ada_w.shape[0] == 1, "single-layer encoder"
    p = dict(norm1_g=norm1_g[0], norm2_g=norm2_g[0], w_in=_bf(w_in[0]), na_bias=_na_bias_table(na_rpb[0]),
             rwkv_mu=rwkv_mu[0], rwkv_w0=rwkv_w0[0], rwkv_w2=rwkv_w2[0], rwkv_a0=rwkv_a0[0],
             rwkv_a2=rwkv_a2[0], rwkv_g2=rwkv_g2[0], rwkv_kk=rwkv_kk[0], rwkv_ka=rwkv_ka[0],
             rwkv_rk=rwkv_rk[0].reshape(-1), rwkv_lnx_g=rwkv_lnx_g[0], rwkv_lnx_b=rwkv_lnx_b[0],
             w_out=_bf(w_out[0]), router_g_w=router_g_w[0], router_g_b=router_g_b[0],
             router_e_w=router_e_w[0], router_e_b=router_e_b[0], exp_w1=_bf(exp_w1[0]),
             exp_w3=_bf(exp_w3[0]), exp_w2=_bf(exp_w2[0]), final_g=final_g)
    nb = x_prompt.shape[0]
    mod = _modulation(jnp.concatenate([c_prompt, c_sample], axis=0), ada_w[0], ada_b[0])
    mod = mod.reshape(mod.shape[0], 6, D_MODEL)
    return (_trunk(x_prompt, mod[:nb], p), _trunk(x_sample, mod[nb:], p))
```

```python
import functools

import numpy as np
import jax
import jax.numpy as jnp
from jax import lax
from jax.experimental import pallas as pl
from jax.experimental.pallas import tpu as pltpu

F32 = jnp.float32
BF16 = jnp.bfloat16

D_MODEL = 1024
GRID_W = 64
N_HEADS = 8
HEAD_DIM = 64
MIX_W = N_HEADS * HEAD_DIM
NA_KH = 8
NA_KW = 16
RWKV_IN = 3 * MIX_W + 4 * 64 + 128
IN_WIDTH = 3 * MIX_W + RWKV_IN
N_GROUPS = 4
EXPERTS_PER_GROUP = 8
N_EXPERTS = N_GROUPS * EXPERTS_PER_GROUP
EXPERT_HIDDEN = 512
NORM_EPS = 1e-6
LNX_EPS = 64e-5
MASK_VALUE = -1e30

LANES = 128
N_SLABS = MIX_W // LANES
CHUNK = 64
VMEM_LIMIT = 56 * 1024 * 1024

TM_PROJ = 256
TP_PREP = 256
SCAN_ROWS = 256
TM_MOE = 1024
MOE_BLK = 128
ROUTER_ROWS = 40


def _bf(x):
    return x.astype(BF16)


def _dot(a, b):
    return jnp.dot(a, b, preferred_element_type=F32)


def _dot_nt(a, b):
    return lax.dot_general(a, b, (((1,), (1,)), ((), ())), preferred_element_type=F32)


def _dot_tn(a, b):
    return lax.dot_general(a, b, (((0,), (0,)), ((), ())), preferred_element_type=F32)


def _split(x):
    hi = _bf(x)
    return hi, _bf(x - hi.astype(F32))


def _dot_hilo(x, m_bf):
    hi, lo = _split(x)
    return _dot(hi, m_bf) + _dot(lo, m_bf)


def _ldot_hilo(m_bf, x):
    hi, lo = _split(x)
    return _dot(m_bf, hi) + _dot(m_bf, lo)


def _params(*sem):
    return pltpu.CompilerParams(dimension_semantics=sem, vmem_limit_bytes=VMEM_LIMIT)


def _full(shape):
    n = len(shape)
    return pl.BlockSpec(shape, lambda *_: (0,) * n)


def _mod_kernel(c_ref, w_ref, b_ref, o_ref):
    c = c_ref[...]
    s = c * jax.nn.sigmoid(c)
    o_ref[...] = jnp.dot(s, w_ref[...], precision=lax.Precision.HIGHEST,
                         preferred_element_type=F32) + b_ref[...]


def _modulation(c, ada_w, ada_b):
    nb = c.shape[0]
    tn = 1024
    return pl.pallas_call(
        _mod_kernel,
        grid=(6 * D_MODEL // tn,),
        in_specs=[_full((nb, D_MODEL)),
                  pl.BlockSpec((D_MODEL, tn), lambda j: (0, j)),
                  pl.BlockSpec((1, tn), lambda j: (0, j))],
        out_specs=pl.BlockSpec((nb, tn), lambda j: (0, j)),
        out_shape=jax.ShapeDtypeStruct((nb, 6 * D_MODEL), F32),
        compiler_params=_params("arbitrary"),
        name="mod",
    )(c, ada_w, ada_b.reshape(1, -1))


def _rms(x):
    return x * lax.rsqrt(jnp.mean(x * x, axis=-1, keepdims=True) + NORM_EPS)


def _inproj_kernel(x_ref, mod_ref, g_ref, w_ref, q_ref, k_ref, v_ref, zr_ref):
    h = _rms(x_ref[0]) * g_ref[...]
    h = h * (1.0 + mod_ref[0, 1:2, :]) + mod_ref[0, 0:1, :]
    z = _dot(_bf(h), w_ref[...])
    q_ref[0] = _bf(z[:, 0:MIX_W] * (HEAD_DIM ** -0.5))
    k_ref[0] = _bf(z[:, MIX_W:2 * MIX_W])
    v_ref[0] = _bf(z[:, 2 * MIX_W:3 * MIX_W])
    zr_ref[0] = z[:, 3 * MIX_W:]


def _inproj(x, mod, norm_g, w_in_bf):
    b, t, d = x.shape
    tm = TM_PROJ
    tok = lambda w: pl.BlockSpec((1, tm, w), lambda i, j: (i, j, 0))
    return pl.pallas_call(
        _inproj_kernel,
        grid=(b, t // tm),
        in_specs=[tok(d),
                  pl.BlockSpec((1, 6, d), lambda i, j: (i, 0, 0)),
                  _full((1, d)),
                  _full((d, IN_WIDTH))],
        out_specs=[tok(MIX_W), tok(MIX_W), tok(MIX_W), tok(RWKV_IN)],
        out_shape=[jax.ShapeDtypeStruct((b, t, MIX_W), BF16)] * 3
        + [jax.ShapeDtypeStruct((b, t, RWKV_IN), F32)],
        compiler_params=_params("parallel", "arbitrary"),
        name="inproj",
    )(x, mod, norm_g.reshape(1, d), w_in_bf)


def _na_bias_table(rpb):
    qc = np.arange(GRID_W)[:, None]
    kc = np.arange(GRID_W)[None, :]
    win = np.clip(qc - NA_KW // 2, 0, GRID_W - NA_KW)
    valid = (kc >= win) & (kc < win + NA_KW)
    dx = np.clip(kc - qc + NA_KW - 1, 0, 2 * NA_KW - 2)
    dy = np.arange(NA_KH)[:, None] + np.arange(NA_KH)[None, :]
    tab = rpb.astype(F32)[:, dy][:, :, :, dx]
    tab = jnp.where(jnp.asarray(valid)[None, None, None], tab, MASK_VALUE)
    tab = jnp.transpose(tab, (0, 1, 3, 2, 4))
    return tab.reshape(N_HEADS, NA_KH, GRID_W, NA_KH * GRID_W)


def _na_kernel(q_ref, k_ref, v_ref, bias_ref, o_ref, *, rows):
    lane = lax.broadcasted_iota(jnp.int32, (GRID_W, LANES), 1)
    first = lane < HEAD_DIM
    nk = NA_KH * GRID_W

    def row_body(i, carry):
        rs = jnp.clip(i - NA_KH // 2, 0, rows - NA_KH)
        d0 = rs - i + (NA_KH - 1)
        q0 = pl.multiple_of(i * GRID_W, GRID_W)
        k0 = pl.multiple_of(rs * GRID_W, GRID_W)
        slabs = [slice(p * LANES, (p + 1) * LANES) for p in range(N_SLABS)]
        heads = [(p, hh) for p in range(N_SLABS) for hh in range(2)]
        q2 = [q_ref[0, pl.ds(q0, GRID_W), sl].astype(F32) for sl in slabs]
        k8 = [k_ref[0, pl.ds(k0, nk), sl] for sl in slabs]
        v8 = [v_ref[0, pl.ds(k0, nk), sl] for sl in slabs]
        qm = [_bf(jnp.where(first if hh == 0 else ~first, q2[p], 0.0)) for p, hh in heads]
        s = [_dot_nt(qm[i], k8[p]) + bias_ref[2 * p + hh, d0] for i, (p, hh) in enumerate(heads)]
        e = [jnp.exp(x - jnp.max(x, axis=-1, keepdims=True)) for x in s]
        l = [jnp.sum(x, axis=-1, keepdims=True) for x in e]
        o = [_dot(_bf(e[i]), v8[p]) / l[i] for i, (p, hh) in enumerate(heads)]
        for p, sl in enumerate(slabs):
            o_ref[0, pl.ds(q0, GRID_W), sl] = _bf(jnp.where(first, o[2 * p], o[2 * p + 1]))
        return carry

    lax.fori_loop(0, rows, row_body, 0)


def _neighbourhood_attention(q, k, v, bias):
    b, t, w = q.shape
    rows = t // GRID_W
    assert rows >= NA_KH
    seq = pl.BlockSpec((1, t, w), lambda i: (i, 0, 0))
    return pl.pallas_call(
        functools.partial(_na_kernel, rows=rows),
        grid=(b,),
        in_specs=[seq, seq, seq, pl.BlockSpec(memory_space=pltpu.VMEM)],
        out_specs=seq,
        out_shape=jax.ShapeDtypeStruct((b, t, w), BF16),
        compiler_params=_params("parallel"),
        name="na",
    )(q, k, v, bias)


def _prep_kernel(z_ref, zp_ref, zn_ref, mu_ref, w0_ref, w2_ref, a0_ref, a2_ref, g2_ref,
                 kk_ref, ka_ref, rk_ref, ho_ref, trif_ref, trib_ref, ones_ref,
                 rhf_ref, ahf_ref, bhf_ref, khf_ref, btf_ref, ktf_ref,
                 rhb_ref, ahb_ref, bhb_ref, khb_ref, btb_ref, ktb_ref,
                 v_ref, bonus_ref, g_ref, wc_ref, *, n_tiles):
    j = pl.program_id(1)
    tp = z_ref.shape[1]
    row = lax.broadcasted_iota(jnp.int32, (tp, 1), 0)
    prev_row = jnp.where(j > 0, zp_ref[0, 7:8, :], 0.0)
    next_row = jnp.where(j < n_tiles - 1, zn_ref[0, 0:1, :], 0.0)

    def shifted(c0, c1):
        zc = z_ref[0, :, c0:c1]
        prev = jnp.where(row == 0, prev_row[:, c0:c1], pltpu.roll(zc, 1, 0))
        nxt = jnp.where(row == tp - 1, next_row[:, c0:c1], pltpu.roll(zc, tp - 1, 0))
        return zc + mu_ref[0:1, c0:c1] * (prev - zc) + mu_ref[1:2, c0:c1] * (nxt - zc)

    r = shifted(0, MIX_W)
    k = shifted(MIX_W, 2 * MIX_W)
    v = shifted(2 * MIX_W, 3 * MIX_W)
    xw = shifted(3 * MIX_W, 3 * MIX_W + LANES)
    xa = shifted(3 * MIX_W + LANES, 3 * MIX_W + 2 * LANES)
    xg = shifted(3 * MIX_W + 2 * LANES, RWKV_IN)
    ho = ho_ref[...]

    g_ref[0] = _dot(_bf(jax.nn.sigmoid(xg)), g2_ref[...])
    v_ref[0] = _bf(v)
    kk = k * kk_ref[...]
    kk = kk / jnp.maximum(jnp.sqrt(_dot_hilo(kk * kk, ho)), 1e-12)
    tw = _bf(jnp.tanh(xw))
    xab = _bf(xa)

    kd_sum = jnp.zeros_like(k)
    dirs = ((trif_ref, (rhf_ref, ahf_ref, bhf_ref, khf_ref, btf_ref, ktf_ref)),
            (trib_ref, (rhb_ref, ahb_ref, bhb_ref, khb_ref, btb_ref, ktb_ref)))
    for d, (tri_ref, outs) in enumerate(dirs):
        y = w0_ref[d:d + 1, :] + _dot(tw, w2_ref[d])
        sp = jnp.maximum(-y, 0.0) + jnp.log1p(jnp.exp(-jnp.abs(y)))
        ld = -jnp.exp(-sp - 0.5)
        a = jax.nn.sigmoid(a0_ref[d:d + 1, :] + _dot(xab, a2_ref[d]))
        kd = k * (1.0 + (a - 1.0) * ka_ref[...])
        kd_sum = kd_sum + kd
        cum = _ldot_hilo(tri_ref[...], ld)
        tot = _ldot_hilo(ones_ref[...], ld)
        w_inv = jnp.exp(-cum)
        w_rest = jnp.exp(tot - cum)
        b = kk * a
        outs[0][0] = _bf(r * jnp.exp(cum))
        outs[1][0] = _bf(-kk * jnp.exp(cum - ld))
        outs[2][0] = _bf(b * w_inv)
        outs[3][0] = _bf(kd * w_inv)
        outs[4][0] = _bf(b * w_rest)
        outs[5][0] = _bf(kd * w_rest)
        for c in range(tp // CHUNK):
            wc_ref[0, c, d:d + 1, :] = jnp.exp(tot[c * CHUNK:c * CHUNK + 1, :])
    bonus_ref[0] = _dot_hilo(r * kd_sum * rk_ref[...], ho) * v


def _chunk_matrices(tp):
    t = np.arange(tp)
    same = (t[:, None] // CHUNK) == (t[None, :] // CHUNK)
    trif = same & (t[None, :] <= t[:, None])
    trib = same & (t[None, :] >= t[:, None])
    return [jnp.asarray(m, BF16) for m in (trif, trib, same)]


def _head_ones():
    h = np.arange(MIX_W) // HEAD_DIM
    return jnp.asarray(h[:, None] == h[None, :], BF16)


def _rwkv_prep(zr, mu, w0, w2, a0, a2, g2, k_k, k_a, r_k):
    b, t, _ = zr.shape
    tp = TP_PREP
    n_tiles = t // tp
    nc = t // CHUNK
    zpad = jnp.zeros((2, 64, MIX_W), F32)
    w2p = _bf(jnp.stack([jnp.concatenate([w2[0], zpad[0]], 0), jnp.concatenate([zpad[0], w2[1]], 0)]))
    a2p = _bf(jnp.stack([jnp.concatenate([a2[0], zpad[0]], 0), jnp.concatenate([zpad[0], a2[1]], 0)]))
    trif, trib, ones = _chunk_matrices(tp)
    hb = tp // 8
    tok = lambda w: pl.BlockSpec((1, tp, w), lambda i, j: (i, j, 0))
    row = lambda a: a.reshape(1, MIX_W)
    outs = pl.pallas_call(
        functools.partial(_prep_kernel, n_tiles=n_tiles),
        grid=(b, n_tiles),
        in_specs=[tok(RWKV_IN),
                  pl.BlockSpec((1, 8, RWKV_IN), lambda i, j: (i, jnp.maximum(j * hb - 1, 0), 0)),
                  pl.BlockSpec((1, 8, RWKV_IN), lambda i, j: (i, jnp.minimum((j + 1) * hb, t // 8 - 1), 0)),
                  _full((2, RWKV_IN)), _full((2, MIX_W)), _full((2, LANES, MIX_W)),
                  _full((2, MIX_W)), _full((2, LANES, MIX_W)), _full((LANES, MIX_W)),
                  _full((1, MIX_W)), _full((1, MIX_W)), _full((1, MIX_W)),
                  _full((MIX_W, MIX_W)), _full((tp, tp)), _full((tp, tp)), _full((tp, tp))],
        out_specs=[tok(MIX_W)] * 13 + [tok(MIX_W), tok(MIX_W),
                   pl.BlockSpec((1, tp // CHUNK, 2, MIX_W), lambda i, j: (i, j, 0, 0))],
        out_shape=[jax.ShapeDtypeStruct((b, t, MIX_W), BF16)] * 13
        + [jax.ShapeDtypeStruct((b, t, MIX_W), F32)] * 2
        + [jax.ShapeDtypeStruct((b, nc, 2, MIX_W), F32)],
        compiler_params=_params("parallel", "arbitrary"),
        name="rwkv_prep",
    )(zr, zr, zr, mu, w0, w2p, a0, a2p, _bf(g2), row(k_k), row(k_a), row(r_k),
      _head_ones(), trif, trib, ones)
    return outs


N_INV_LEVELS = CHUNK.bit_length() - 1


def _inverse_level_masks():
    idx = np.arange(LANES)
    t = idx % CHUNK
    same_head = (idx[:, None] // CHUNK) == (idx[None, :] // CHUNK)
    out = np.zeros((2, N_INV_LEVELS, LANES, LANES), np.float32)
    for lvl in range(N_INV_LEVELS):
        s = 1 << lvl
        same_blk = (t[:, None] // (2 * s)) == (t[None, :] // (2 * s))
        hi_r, hi_c = (t[:, None] % (2 * s)) >= s, (t[None, :] % (2 * s)) >= s
        out[0, lvl] = same_head & same_blk & hi_r & ~hi_c
        out[1, lvl] = same_head & same_blk & ~hi_r & hi_c
    return jnp.asarray(out, BF16)


def _scan_kernel(*refs, n_sub):
    fwd_in, bwd_in = refs[0:8], refs[8:16]
    lm_ref, yf_ref, yb_ref, st_ref = refs[16], refs[17], refs[18], refs[19]
    j = pl.program_id(1)
    eye = _bf((lax.broadcasted_iota(jnp.int32, (LANES, LANES), 0)
               == lax.broadcasted_iota(jnp.int32, (LANES, LANES), 1)).astype(F32))

    @pl.when(j == 0)
    def _():
        st_ref[...] = jnp.zeros_like(st_ref)

    lane = lax.broadcasted_iota(jnp.int32, (CHUNK, LANES), 1)
    m_a = _bf((lane < HEAD_DIM).astype(F32))
    m_b = _bf((lane >= HEAD_DIM).astype(F32))
    ti = lax.broadcasted_iota(jnp.int32, (LANES, LANES), 0) % CHUNK
    tj = lax.broadcasted_iota(jnp.int32, (LANES, LANES), 1) % CHUNK

    def stack(z):
        return jnp.concatenate([z * m_a, z * m_b], axis=0)

    ins = (fwd_in, bwd_in)
    y_refs = (yf_ref, yb_ref)
    strict = (ti > tj, ti < tj)
    incl = (ti >= tj, ti <= tj)
    chains = [(d, p) for d in range(2) for p in range(N_SLABS)]

    def sub(c, carry):
        cs = (c, n_sub - 1 - c)
        r0 = [pl.multiple_of(cc * CHUNK, CHUNK) for cc in cs]

        def ld(i):
            return [stack(ins[d][i][0, pl.ds(r0[d], CHUNK), p * LANES:(p + 1) * LANES]) for d, p in chains]

        r_s, a_s, b_s, k_s, bt_s, kt_s, v_s = (ld(i) for i in range(7))
        q2 = [st_ref[d, p] for d, p in chains]
        lhs = [jnp.concatenate([a, r], axis=0) for a, r in zip(a_s, r_s)]
        lp = [_dot_nt(l, _bf(q)) for l, q in zip(lhs, q2)]
        aa = [_dot_nt(l, jnp.concatenate([b, k], axis=0)) for l, b, k in zip(lhs, b_s, k_s)]
        nb = [_bf(jnp.where(strict[d], m[0:LANES, 0:LANES], 0.0)) for (d, _), m in zip(chains, aa)]
        n_ak = [_bf(jnp.where(strict[d], m[0:LANES, LANES:], 0.0)) for (d, _), m in zip(chains, aa)]
        n_r = [_bf(jnp.where(jnp.concatenate([incl[d], incl[d]], axis=1), m[LANES:], 0.0))
               for (d, _), m in zip(chains, aa)]
        x = [l[0:LANES] + _dot(n, v) for l, n, v in zip(lp, n_ak, v_s)]
        t_inv = [eye + n * lm_ref[d, 0] for (d, _), n in zip(chains, nb)]
        for lvl in range(1, N_INV_LEVELS):
            de = [_bf(_dot(t, n * lm_ref[d, lvl])) for (d, _), t, n in zip(chains, t_inv, nb)]
            t_inv = [t + _bf(_dot(e, t)) for t, e in zip(t_inv, de)]
        uv = [jnp.concatenate([_bf(_dot(t, _bf(xx))), v], axis=0) for t, xx, v in zip(t_inv, x, v_s)]
        ys = [l[LANES:] + _dot(n, w) for l, n, w in zip(lp, n_r, uv)]
        upd = [_dot_tn(w, jnp.concatenate([b, k], axis=0)) for w, b, k in zip(uv, bt_s, kt_s)]
        for i, (d, p) in enumerate(chains):
            sl = slice(p * LANES, (p + 1) * LANES)
            y_refs[d][0, pl.ds(r0[d], CHUNK), sl] = ys[i][0:CHUNK] + ys[i][CHUNK:]
            wrow = ins[d][7][0, pl.ds(cs[d], 1), d, sl]
            st_ref[d, p] = q2[i] * wrow + upd[i]
        return carry

    lax.fori_loop(0, n_sub, sub, 0)


def _rwkv_scan(prep):
    (rhf, ahf, bhf, khf, btf, ktf, rhb, ahb, bhb, khb, btb, ktb, v, _, _, wc) = prep
    b, t, _ = v.shape
    rows = SCAN_ROWS
    n_sub = rows // CHUNK
    ns = t // rows
    f_tok = pl.BlockSpec((1, rows, MIX_W), lambda i, j: (i, j, 0))
    b_tok = pl.BlockSpec((1, rows, MIX_W), lambda i, j: (i, ns - 1 - j, 0))
    f_wc = pl.BlockSpec((1, n_sub, 2, MIX_W), lambda i, j: (i, j, 0, 0))
    b_wc = pl.BlockSpec((1, n_sub, 2, MIX_W), lambda i, j: (i, ns - 1 - j, 0, 0))
    return pl.pallas_call(
        functools.partial(_scan_kernel, n_sub=n_sub),
        grid=(b, ns),
        in_specs=[f_tok] * 7 + [f_wc] + [b_tok] * 7 + [b_wc] + [_full((2, N_INV_LEVELS, LANES, LANES))],
        out_specs=[f_tok, b_tok],
        out_shape=[jax.ShapeDtypeStruct((b, t, MIX_W), F32)] * 2,
        scratch_shapes=[pltpu.VMEM((2, N_SLABS, LANES, LANES), F32)],
        compiler_params=_params("parallel", "arbitrary"),
        name="rwkv_scan",
    )(rhf, ahf, bhf, khf, btf, ktf, v, wc, rhb, ahb, bhb, khb, btb, ktb, v, wc, _inverse_level_masks())


def _outproj_kernel(att_ref, yf_ref, yb_ref, bonus_ref, g_ref, x_ref, mod_ref, wa_ref, wr_ref,
                    lg_ref, lb_ref, n2_ref, hm_ref, x1_ref, h2_ref):
    wkv = yf_ref[0] + yb_ref[0]
    hm = hm_ref[...]
    mean = _dot_hilo(wkv, hm) * (1.0 / HEAD_DIM)
    dev = wkv - mean
    var = _dot_hilo(dev * dev, hm) * (1.0 / HEAD_DIM)
    rec = dev * lax.rsqrt(var + LNX_EPS) * lg_ref[...] + lb_ref[...]
    rec = (rec + bonus_ref[0]) * g_ref[0]
    o = _dot(att_ref[0], wa_ref[...]) + _dot(_bf(rec), wr_ref[...])
    x1 = x_ref[0] + mod_ref[0, 2:3, :] * o
    x1_ref[0] = x1
    h = _rms(x1) * n2_ref[...]
    h2_ref[0] = _bf(h * (1.0 + mod_ref[0, 4:5, :]) + mod_ref[0, 3:4, :])


def _outproj(att, yf, yb, bonus, g, x, mod, w_out_bf, lnx_g, lnx_b, norm2_g):
    b, t, d = x.shape
    tm = TM_PROJ
    tok = lambda w: pl.BlockSpec((1, tm, w), lambda i, j: (i, j, 0))
    return pl.pallas_call(
        _outproj_kernel,
        grid=(b, t // tm),
        in_specs=[tok(MIX_W)] * 5 + [tok(d), pl.BlockSpec((1, 6, d), lambda i, j: (i, 0, 0)),
                  _full((MIX_W, d)), _full((MIX_W, d)), _full((1, MIX_W)), _full((1, MIX_W)),
                  _full((1, d)), _full((MIX_W, MIX_W))],
        out_specs=[tok(d), tok(d)],
        out_shape=[jax.ShapeDtypeStruct((b, t, d), F32), jax.ShapeDtypeStruct((b, t, d), BF16)],
        compiler_params=_params("parallel", "arbitrary"),
        name="outproj",
    )(att, yf, yb, bonus, g, x, mod, w_out_bf[:MIX_W], w_out_bf[MIX_W:],
      lnx_g.reshape(1, MIX_W), lnx_b.reshape(1, MIX_W), norm2_g.reshape(1, d), _head_ones())


def _first_index_of_max(x, m, n):
    idx = lax.broadcasted_iota(jnp.int32, x.shape, 0)
    return jnp.min(jnp.where(x == m, idx, n), axis=0, keepdims=True)


def _route(logit_t):
    lg = logit_t[0:N_GROUPS]
    mg = jnp.max(lg, axis=0, keepdims=True)
    pg = jnp.exp(lg - mg)
    pg = pg / jnp.sum(pg, axis=0, keepdims=True)
    gw = jnp.max(pg, axis=0, keepdims=True)
    gsel = _first_index_of_max(lg, mg, N_GROUPS)
    le = jnp.zeros((EXPERTS_PER_GROUP, logit_t.shape[1]), F32)
    for gi in range(N_GROUPS):
        rows = logit_t[8 + EXPERTS_PER_GROUP * gi:8 + EXPERTS_PER_GROUP * (gi + 1)]
        le = jnp.where(gsel == gi, rows, le)
    me = jnp.max(le, axis=0, keepdims=True)
    pe = jnp.exp(le - me)
    pe = pe / jnp.sum(pe, axis=0, keepdims=True)
    p1 = jnp.max(pe, axis=0, keepdims=True)
    i1 = _first_index_of_max(pe, p1, EXPERTS_PER_GROUP)
    idx = lax.broadcasted_iota(jnp.int32, pe.shape, 0)
    rest = jnp.where(idx == i1, -1.0, pe)
    p2 = jnp.max(rest, axis=0, keepdims=True)
    i2 = _first_index_of_max(rest, p2, EXPERTS_PER_GROUP)
    ps = p1 + p2
    e1 = gsel * EXPERTS_PER_GROUP + i1
    e2 = gsel * EXPERTS_PER_GROUP + i2
    return e1, e2, gw * (p1 / ps), gw * (p2 / ps)


def _moe_kernel(h2_ref, x1_ref, mod_ref, wrh_ref, wrl_ref, br_ref, tri_ref, fg_ref,
                w1_ref, w3_ref, w2_ref, o_ref, key_ref, wx_ref):
    e = pl.program_id(2)
    tm = h2_ref.shape[1]

    @pl.when(e == 0)
    def _():
        h2 = h2_ref[0]
        logit_t = _dot_nt(wrh_ref[...], h2) + _dot_nt(wrl_ref[...], h2) + br_ref[:, 0:1]
        e1, e2, w1, w2 = _route(logit_t)
        eid = lax.broadcasted_iota(jnp.int32, (N_EXPERTS, tm), 0)
        hit1 = eid == e1
        hit2 = eid == e2
        sel = jnp.where(hit1 | hit2, 1.0, 0.0)
        rank = _dot(_bf(sel), tri_ref[...])
        key_ref[...] = jnp.where(sel > 0.0, rank, -1.0).astype(jnp.int32)
        wx_ref[...] = jnp.where(hit1, w1, jnp.where(hit2, w2, 0.0))
        o_ref[0] = jnp.zeros((tm, D_MODEL), F32)

    key = key_ref[pl.ds(e, 1), :]
    wrow = wx_ref[pl.ds(e, 1), :]
    count = jnp.max(key) + 1
    slot0 = lax.broadcasted_iota(jnp.int32, (MOE_BLK, 1), 0)

    def block(jb, carry):
        hit = key == (slot0 + jb * MOE_BLK)
        gsel = _bf(jnp.where(hit, 1.0, 0.0))
        xb = _bf(_dot(gsel, h2_ref[0]))
        hid = jax.nn.silu(_dot(xb, w1_ref[0])) * _dot(xb, w3_ref[0])
        yb = _dot(_bf(hid), w2_ref[0])
        ws = jnp.sum(jnp.where(hit, wrow, 0.0), axis=1, keepdims=True)
        o_ref[0] += _dot_tn(gsel, _bf(yb * ws))
        return carry

    lax.fori_loop(0, (count + MOE_BLK - 1) // MOE_BLK, block, 0)

    @pl.when(e == N_EXPERTS - 1)
    def _():
        x2 = x1_ref[0] + mod_ref[0, 5:6, :] * o_ref[0]
        o_ref[0] = _rms(x2) * fg_ref[...]


def _moe(h2, x1, mod, rg_w, rg_b, re_w, re_b, w1_bf, w3_bf, w2_bf, final_g):
    b, t, d = x1.shape
    tm = TM_MOE
    wr = jnp.zeros((ROUTER_ROWS, d), F32).at[0:N_GROUPS].set(rg_w.T).at[8:].set(re_w.T)
    wr_hi = _bf(wr)
    wr_lo = _bf(wr - wr_hi.astype(F32))
    br = jnp.zeros((ROUTER_ROWS,), F32).at[0:N_GROUPS].set(rg_b).at[8:].set(re_b)
    br = jnp.broadcast_to(br[:, None], (ROUTER_ROWS, LANES))
    tt = np.arange(tm)
    tri = jnp.asarray(tt[:, None] < tt[None, :], BF16)
    tok = lambda w: pl.BlockSpec((1, tm, w), lambda i, j, e: (i, j, 0))
    return pl.pallas_call(
        _moe_kernel,
        grid=(b, t // tm, N_EXPERTS),
        in_specs=[tok(d), tok(d), pl.BlockSpec((1, 6, d), lambda i, j, e: (i, 0, 0)),
                  _full((ROUTER_ROWS, d)), _full((ROUTER_ROWS, d)), _full((ROUTER_ROWS, LANES)),
                  _full((tm, tm)), _full((1, d)),
                  pl.BlockSpec((1, d, EXPERT_HIDDEN), lambda i, j, e: (e, 0, 0)),
                  pl.BlockSpec((1, d, EXPERT_HIDDEN), lambda i, j, e: (e, 0, 0)),
                  pl.BlockSpec((1, EXPERT_HIDDEN, d), lambda i, j, e: (e, 0, 0))],
        out_specs=tok(d),
        out_shape=jax.ShapeDtypeStruct((b, t, d), F32),
        scratch_shapes=[pltpu.VMEM((N_EXPERTS, tm), jnp.int32), pltpu.VMEM((N_EXPERTS, tm), F32)],
        compiler_params=_params("parallel", "arbitrary", "arbitrary"),
        name="moe",
    )(h2, x1, mod, wr_hi, wr_lo, br, tri, final_g.reshape(1, d), w1_bf, w3_bf, w2_bf)


def _trunk(x, mod, p):
    q, k, v, zr = _inproj(x, mod, p["norm1_g"], p["w_in"])
    att = _neighbourhood_attention(q, k, v, p["na_bias"])
    prep = _rwkv_prep(zr, p["rwkv_mu"], p["rwkv_w0"], p["rwkv_w2"], p["rwkv_a0"], p["rwkv_a2"],
                      p["rwkv_g2"], p["rwkv_kk"], p["rwkv_ka"], p["rwkv_rk"])
    yf, yb = _rwkv_scan(prep)
    x1, h2 = _outproj(att, yf, yb, prep[13], prep[14], x, mod, p["w_out"],
                      p["rwkv_lnx_g"], p["rwkv_lnx_b"], p["norm2_g"])
    return _moe(h2, x1, mod, p["router_g_w"], p["router_g_b"], p["router_e_w"], p["router_e_b"],
                p["exp_w1"], p["exp_w3"], p["exp_w2"], p["final_g"])


def kernel(x_prompt, x_sample, c_prompt, c_sample, ada_w, ada_b, norm1_g, norm2_g, w_in, na_rpb, rwkv_mu, rwkv_w0, rwkv_w2, rwkv_a0, rwkv_a2, rwkv_g2, rwkv_kk, rwkv_ka, rwkv_rk, rwkv_lnx_g, rwkv_lnx_b, w_out, router_g_w, router_g_b, router_e_w, router_e_b, exp_w1, exp_w3, exp_w2, final_g):
    assert ada_w.shape[0] == 1, "single-layer encoder"
    p = dict(norm1_g=norm1_g[0], norm2_g=norm2_g[0], w_in=_bf(w_in[0]), na_bias=_na_bias_table(na_rpb[0]),
             rwkv_mu=rwkv_mu[0], rwkv_w0=rwkv_w0[0], rwkv_w2=rwkv_w2[0], rwkv_a0=rwkv_a0[0],
             rwkv_a2=rwkv_a2[0], rwkv_g2=rwkv_g2[0], rwkv_kk=rwkv_kk[0], rwkv_ka=rwkv_ka[0],
             rwkv_rk=rwkv_rk[0].reshape(-1), rwkv_lnx_g=rwkv_lnx_g[0], rwkv_lnx_b=rwkv_lnx_b[0],
             w_out=_bf(w_out[0]), router_g_w=router_g_w[0], router_g_b=router_g_b[0],
             router_e_w=router_e_w[0], router_e_b=router_e_b[0], exp_w1=_bf(exp_w1[0]),
             exp_w3=_bf(exp_w3[0]), exp_w2=_bf(exp_w2[0]), final_g=final_g)
    nb = x_prompt.shape[0]
    mod = _modulation(jnp.concatenate([c_prompt, c_sample], axis=0), ada_w[0], ada_b[0])
    mod = mod.reshape(mod.shape[0], 6, D_MODEL)
    return (_trunk(x_prompt, mod[:nb], p), _trunk(x_sample, mod[nb:], p))
```
